```python
import jax, jax.numpy as jnp
from jax import lax
import numpy as np

D_MODEL = 1024
BATCH = 8
SEQ = 4096
DEPTH = 2

N_A_LAYERS = DEPTH // 2
N_B_LAYERS = DEPTH - N_A_LAYERS
RMS_EPS = 1e-6
MEM_LEN = 256
MEM_HEADS = 4
MEM_HEAD_DIM = 64
MEM_WIDTH = MEM_HEADS * MEM_HEAD_DIM
HGRN_EXPAND = 128
HGRN_HEAD_V = 128
HGRN_HEADS = (D_MODEL - MEM_WIDTH) // HGRN_HEAD_V
HGRN_F = HGRN_HEADS * HGRN_EXPAND
HGRN_V = HGRN_HEADS * HGRN_HEAD_V
HGRN_CHUNK = 64
A_IN_WIDTH = 2 * HGRN_F + 2 * HGRN_V + MEM_WIDTH
SB_HEAD_DIM = 64
SB_WIDTH = D_MODEL - MEM_WIDTH
SB_HEADS = SB_WIDTH // SB_HEAD_DIM
SB_BLOCK = 128
B_IN_WIDTH = SB_WIDTH + MEM_WIDTH
MIX_WIDTH = D_MODEL
D_FF_DENSE = 2816
N_EXPERTS = 8
TOP_K = 2
D_FF_EXPERT = 3584
MOE_BLOCK = 512
N_DENSE_FFN = (DEPTH + 1) // 2
N_MOE_FFN = DEPTH // 2

kernel_name = "yoco_hgrn2_stickbreaking_moe_block"


def rms_norm(x, g):
    xf = x.astype(jnp.float32)
    var = jnp.mean(xf * xf, axis=-1, keepdims=True)
    return (xf * lax.rsqrt(var + RMS_EPS) * g.astype(jnp.float32)).astype(x.dtype)


def hgrn2_mix(q, f_logit, i, lower_bound, norm_g):
    B, T, _ = q.shape
    H, dk, dv, C = HGRN_HEADS, HGRN_EXPAND, HGRN_HEAD_V, HGRN_CHUNK
    n = T // C

    def heads(t, d):
        return t.astype(jnp.float32).reshape(B, n, C, H, d).transpose(0, 3, 1, 2, 4)

    f = lower_bound + (1.0 - lower_bound) * jax.nn.sigmoid(f_logit.astype(jnp.float32))
    qh = heads(jax.nn.silu(q.astype(jnp.float32)), dk)
    kh = heads(1.0 - f, dk)
    gh = heads(jnp.log(f), dk)
    vh = heads(i, dv)
    b = jnp.cumsum(gh, axis=-2)
    b_last = b[..., -1:, :]
    q_dec = qh * jnp.exp(b)
    k_intra = kh * jnp.exp(-b)
    k_state = kh * jnp.exp(b_last - b)
    causal = jnp.tril(jnp.ones((C, C), dtype=bool))
    scores = jnp.where(causal, jnp.einsum('bhnck,bhnsk->bhncs', q_dec, k_intra), 0.0)
    o_intra = jnp.einsum('bhncs,bhnsv->bhncv', scores, vh)

    def step(S, inp):
        qd, ks, vv, dec = inp
        o = jnp.einsum('bhck,bhkv->bhcv', qd, S)
        S = dec[..., None] * S + jnp.einsum('bhck,bhcv->bhkv', ks, vv)
        return S, o

    xs = (jnp.moveaxis(q_dec, 2, 0), jnp.moveaxis(k_state, 2, 0), jnp.moveaxis(vh, 2, 0),
          jnp.moveaxis(jnp.exp(b_last[..., 0, :]), 2, 0))
    S0 = jnp.zeros((B, H, dk, dv), jnp.float32)
    _, o_inter = lax.scan(step, S0, xs)
    o = o_intra + jnp.moveaxis(o_inter, 0, 2)
    o = rms_norm(o, norm_g)
    return o.transpose(0, 2, 3, 1, 4).reshape(B, T, H * dv)


def stick_breaking_attention(q, k, v):
    B, H, T, d = q.shape
    scale = d ** -0.5
    outs = []
    for blk in range(T // SB_BLOCK):
        lo, hi = blk * SB_BLOCK, (blk + 1) * SB_BLOCK
        qb = q[:, :, lo:hi].astype(jnp.float32)
        kb = k[:, :, :hi].astype(jnp.float32)
        vb = v[:, :, :hi].astype(jnp.float32)
        z = jnp.einsum('bhqd,bhsd->bhqs', qb, kb) * scale
        strict = jnp.arange(hi)[None, :] < jnp.arange(lo, hi)[:, None]
        log_beta = jax.nn.log_sigmoid(z)
        log_keep = jnp.where(strict, log_beta - z, 0.0)
        later = lax.cumsum(log_keep, axis=log_keep.ndim - 1, reverse=True) - log_keep
        w = jnp.where(strict, jnp.exp(log_beta + later), 0.0)
        outs.append(jnp.einsum('bhqs,bhsd->bhqd', w, vb))
    return jnp.concatenate(outs, axis=2)


def memory_attention(q, mem_k, mem_v):
    B, T = q.shape[:2]
    s = jnp.einsum('bthd,bmhd->bhtm', q.astype(jnp.float32), mem_k.astype(jnp.float32)) * (MEM_HEAD_DIM ** -0.5)
    p = jax.nn.softmax(s, axis=-1)
    o = jnp.einsum('bhtm,bmhd->bthd', p, mem_v.astype(jnp.float32))
    return o.reshape(B, T, MEM_WIDTH)


def swiglu(x, w_gate_up, w_down):
    g, u = jnp.split(x @ w_gate_up, 2, axis=-1)
    return (jax.nn.silu(g) * u) @ w_down


def moe_swiglu(x, w_router, w_gate_up, w_down):
    B, T, D = x.shape
    N = B * T
    NK = N * TOP_K
    xf = x.reshape(N, D)
    logits = jnp.dot(xf.astype(jnp.float32), w_router.astype(jnp.float32))
    top_logit, top_idx = lax.top_k(logits, TOP_K)
    gates = jax.nn.softmax(top_logit, axis=-1)
    expert = top_idx.reshape(-1)
    token = jnp.arange(NK, dtype=jnp.int32) // TOP_K
    order = jnp.argsort(expert)
    e_sorted = expert[order]
    tok_sorted = token[order]
    gate_sorted = gates.reshape(-1)[order]
    counts = jnp.bincount(expert, length=N_EXPERTS)
    starts = jnp.cumsum(counts) - counts
    padded = (counts + MOE_BLOCK - 1) // MOE_BLOCK * MOE_BLOCK
    padded_ends = jnp.cumsum(padded)
    padded_starts = padded_ends - padded
    dest = padded_starts[e_sorted] + jnp.arange(NK, dtype=jnp.int32) - starts[e_sorted]
    n_blocks = -(-NK // MOE_BLOCK) + N_EXPERTS
    x_buf = jnp.zeros((n_blocks * MOE_BLOCK, D), x.dtype).at[dest].set(xf[tok_sorted])
    block_start = jnp.arange(n_blocks, dtype=jnp.int32) * MOE_BLOCK
    block_expert = jnp.minimum(jnp.searchsorted(padded_ends, block_start, side='right'), N_EXPERTS - 1)

    def expert_block(args):
        xb, e = args
        return swiglu(xb, w_gate_up[e], w_down[e])

    y_buf = lax.map(expert_block, (x_buf.reshape(n_blocks, MOE_BLOCK, D), block_expert))
    y_sorted = y_buf.reshape(-1, D)[dest] * gate_sorted[:, None].astype(x.dtype)
    y = jax.ops.segment_sum(y_sorted, tok_sorted, num_segments=N)
    return y.reshape(B, T, D)


def setup_inputs(seed: int = 0) -> dict:
    key = jax.random.key(seed)
    ks = jax.random.split(key, 24)

    def w(k, shape, fan_in):
        return jax.random.normal(k, shape, jnp.float32) * fan_in ** -0.5

    def gain(k, shape):
        return 1.0 + 0.02 * jax.random.normal(k, shape, jnp.float32)

    return {
        "x": jax.random.normal(ks[0], (BATCH, SEQ, D_MODEL), jnp.float32),
        "mem": jax.random.normal(ks[1], (BATCH, MEM_LEN, D_MODEL), jnp.float32),
        "attn_norm_pre": gain(ks[2], (DEPTH, D_MODEL)),
        "attn_norm_post": gain(ks[3], (DEPTH, D_MODEL)),
        "ffn_norm_pre": gain(ks[4], (DEPTH, D_MODEL)),
        "ffn_norm_post": gain(ks[5], (DEPTH, D_MODEL)),
        "w_in_a": w(ks[6], (N_A_LAYERS, D_MODEL, A_IN_WIDTH), D_MODEL),
        "hgrn_lower_bounds": 0.1 * jax.random.normal(ks[7], (N_A_LAYERS + 1, HGRN_F), jnp.float32),
        "hgrn_out_norm": gain(ks[8], (N_A_LAYERS, HGRN_HEAD_V)),
        "w_in_b": w(ks[9], (N_B_LAYERS, D_MODEL, B_IN_WIDTH), D_MODEL),
        "kv_norm": gain(ks[10], (D_MODEL,)),
        "w_kv": w(ks[11], (D_MODEL, 2 * SB_WIDTH), D_MODEL),
        "mem_norm": gain(ks[12], (DEPTH, D_MODEL)),
        "w_mem_kv": w(ks[13], (DEPTH, D_MODEL, 2 * MEM_WIDTH), D_MODEL),
        "w_out": w(ks[14], (DEPTH, MIX_WIDTH, D_MODEL), MIX_WIDTH),
        "w_ffn_gate_up": w(ks[15], (N_DENSE_FFN, D_MODEL, 2 * D_FF_DENSE), D_MODEL),
        "w_ffn_down": w(ks[16], (N_DENSE_FFN, D_FF_DENSE, D_MODEL), D_FF_DENSE),
        "w_router": w(ks[17], (N_MOE_FFN, D_MODEL, N_EXPERTS), D_MODEL),
        "w_exp_gate_up": w(ks[18], (N_MOE_FFN, N_EXPERTS, D_MODEL, 2 * D_FF_EXPERT), D_MODEL),
        "w_exp_down": w(ks[19], (N_MOE_FFN, N_EXPERTS, D_FF_EXPERT, D_MODEL), D_FF_EXPERT),
    }


def reference(x, mem, attn_norm_pre, attn_norm_post, ffn_norm_pre, ffn_norm_post,
              w_in_a, hgrn_lower_bounds, hgrn_out_norm, w_in_b, kv_norm, w_kv,
              mem_norm, w_mem_kv, w_out, w_ffn_gate_up, w_ffn_down,
              w_router, w_exp_gate_up, w_exp_down):
    B, T, D = x.shape
    M = mem.shape[1]
    lower_bounds = jnp.cumsum(jax.nn.softmax(hgrn_lower_bounds.astype(jnp.float32), axis=0), axis=0)
    h = x
    sb_k = None
    sb_v = None
    for layer in range(DEPTH):
        u = rms_norm(h, attn_norm_pre[layer])
        mkv = rms_norm(mem, mem_norm[layer]) @ w_mem_kv[layer]
        mem_k, mem_v = jnp.split(mkv.reshape(B, M, 2 * MEM_HEADS, MEM_HEAD_DIM), 2, axis=2)
        if layer < N_A_LAYERS:
            proj = u @ w_in_a[layer]
            q, f, i, og, mq = jnp.split(proj, [HGRN_F, 2 * HGRN_F, 2 * HGRN_F + HGRN_V,
                                               2 * HGRN_F + 2 * HGRN_V], axis=-1)
            o_main = hgrn2_mix(q, f, i, lower_bounds[layer], hgrn_out_norm[layer]) * jax.nn.silu(og.astype(jnp.float32))
        else:
            proj = u @ w_in_b[layer - N_A_LAYERS]
            sq, mq = jnp.split(proj, [SB_WIDTH], axis=-1)
            sq = sq.reshape(B, T, SB_HEADS, SB_HEAD_DIM).transpose(0, 2, 1, 3)
            o_main = stick_breaking_attention(sq, sb_k, sb_v).transpose(0, 2, 1, 3).reshape(B, T, SB_WIDTH)
        o_mem = memory_attention(mq.reshape(B, T, MEM_HEADS, MEM_HEAD_DIM), mem_k, mem_v)
        mixed = jnp.concatenate([o_main, o_mem], axis=-1).astype(h.dtype) @ w_out[layer]
        h = h + rms_norm(mixed, attn_norm_post[layer])
        u = rms_norm(h, ffn_norm_pre[layer])
        if layer % 2 == 0:
            y = swiglu(u, w_ffn_gate_up[layer // 2], w_ffn_down[layer // 2])
        else:
            y = moe_swiglu(u, w_router[layer // 2], w_exp_gate_up[layer // 2], w_exp_down[layer // 2])
        h = h + rms_norm(y, ffn_norm_post[layer])
        if layer == N_A_LAYERS - 1:
            kv = (rms_norm(h, kv_norm) @ w_kv).reshape(B, T, 2 * SB_HEADS, SB_HEAD_DIM).transpose(0, 2, 1, 3)
            sb_k, sb_v = jnp.split(kv, 2, axis=1)
    return h
```

```python
import functools

import jax
import jax.numpy as jnp
from jax import lax
from jax.experimental import pallas as pl
from jax.experimental.pallas import tpu as pltpu

F32 = jnp.float32
BF16 = jnp.bfloat16
I32 = jnp.int32
U32 = jnp.uint32

RMS_EPS = 1e-6
LANES = 128
SUBLANES = 8
HEAD_DIM = 64
HGRN_HEAD = 128
HGRN_CHUNK = 64
MOE_ROWS = 512
VMEM_LIMIT = 56 * 1024 * 1024

_NT = (((1,), (1,)), ((), ()))


def _params(*sem):
    return pltpu.CompilerParams(dimension_semantics=sem, vmem_limit_bytes=VMEM_LIMIT)


def _resident(shape):
    zeros = (0,) * len(shape)
    return pl.BlockSpec(shape, lambda *_: zeros, pipeline_mode=pl.Buffered(1))


def _rms(xf, g):
    var = jnp.mean(xf * xf, axis=-1, keepdims=True)
    return xf * lax.rsqrt(var + RMS_EPS) * g


def _silu(x):
    return x * jax.nn.sigmoid(x)


def _head_mask(hh):
    lane = lax.broadcasted_iota(I32, (1, LANES), 1)
    return (lane >= hh * HEAD_DIM) & (lane < (hh + 1) * HEAD_DIM)


def _norm_matmul_kernel(x_ref, g_ref, w_ref, *out_refs, splits):
    xn = _rms(x_ref[...], g_ref[...]).astype(BF16)
    off = 0
    for o_ref, width in zip(out_refs, splits):
        for c in range(0, width, 512):
            cw = min(512, width - c)
            o_ref[:, c:c + cw] = jnp.dot(
                xn, w_ref[:, off + c:off + c + cw], preferred_element_type=F32).astype(o_ref.dtype)
        off += width


def norm_matmul(x, g, w, splits, out_dtypes, tm, name):
    n, d = x.shape
    tm = min(tm, n)
    assert n % tm == 0 and sum(splits) == w.shape[1]
    return pl.pallas_call(
        functools.partial(_norm_matmul_kernel, splits=tuple(splits)),
        out_shape=[jax.ShapeDtypeStruct((n, s), dt) for s, dt in zip(splits, out_dtypes)],
        grid=(n // tm,),
        in_specs=[pl.BlockSpec((tm, d), lambda i: (i, 0)),
                  _resident((1, d)),
                  _resident(w.shape)],
        out_specs=[pl.BlockSpec((tm, s), lambda i: (i, 0)) for s in splits],
        compiler_params=_params("parallel"),
        name=name,
    )(x, g.reshape(1, d), w)


def _hgrn_kernel(q_ref, f_ref, i_ref, og_ref, lb_ref, ng_ref, o_ref, st_ref, *, n_chunks, heads):
    C, dh = HGRN_CHUNK, HGRN_HEAD

    @pl.when(pl.program_id(1) == 0)
    def _():
        st_ref[...] = jnp.zeros_like(st_ref)

    lb = lb_ref[...]
    ng = ng_ref[...]
    row = lax.broadcasted_iota(I32, (C, C), 0)
    col = lax.broadcasted_iota(I32, (C, C), 1)
    causal = row >= col
    tri = causal.astype(BF16)

    for c in range(n_chunks):
        sl = pl.ds(c * C, C)
        fg = lb + (1.0 - lb) * jax.nn.sigmoid(f_ref[0, sl, :])
        kh = 1.0 - fg
        gh = jnp.log(fg)
        g1 = gh.astype(BF16)
        r1 = gh - g1.astype(F32)
        g2 = r1.astype(BF16)
        g3 = (r1 - g2.astype(F32)).astype(BF16)
        b = (jnp.dot(tri, g1, preferred_element_type=F32)
             + jnp.dot(tri, g2, preferred_element_type=F32)
             + jnp.dot(tri, g3, preferred_element_type=F32))
        b_last = b[C - 1:C, :]
        q_dec = (_silu(q_ref[0, sl, :].astype(F32)) * jnp.exp(b)).astype(BF16)
        k_intra = (kh * jnp.exp(-b)).astype(BF16)
        k_state = (kh * jnp.exp(b_last - b)).astype(BF16)
        dec = jnp.exp(b_last)
        v = i_ref[0, sl, :].astype(F32)
        og = og_ref[0, sl, :].astype(F32)
        for h in range(heads):
            hs = slice(h * dh, (h + 1) * dh)
            qd_h = q_dec[:, hs]
            v_h = v[:, hs]
            v_hb = v_h.astype(BF16)
            scores = lax.dot_general(qd_h, k_intra[:, hs], _NT, preferred_element_type=F32)
            scores = jnp.where(causal, scores, 0.0).astype(BF16)
            st = st_ref[h]
            o = (jnp.dot(scores, v_hb, preferred_element_type=F32)
                 + lax.dot_general(qd_h, st.astype(BF16), _NT, preferred_element_type=F32))
            st_ref[h] = st * dec[:, hs] + jnp.dot(
                v_h.T.astype(BF16), k_state[:, hs], preferred_element_type=F32)
            o = _rms(o, ng) * _silu(og[:, hs])
            o_ref[0, sl, hs] = o.astype(o_ref.dtype)


def hgrn_mix(q, f, i, og, lb, ng, tt):
    bsz, t, w = q.shape
    heads = w // HGRN_HEAD
    tt = min(tt, t)
    assert t % tt == 0 and tt % HGRN_CHUNK == 0
    blk = pl.BlockSpec((1, tt, w), lambda b, s: (b, s, 0))
    return pl.pallas_call(
        functools.partial(_hgrn_kernel, n_chunks=tt // HGRN_CHUNK, heads=heads),
        out_shape=jax.ShapeDtypeStruct((bsz, t, w), BF16),
        grid=(bsz, t // tt),
        in_specs=[blk, blk, blk, blk, _resident((1, w)), _resident((1, HGRN_HEAD))],
        out_specs=blk,
        scratch_shapes=[pltpu.VMEM((heads, HGRN_HEAD, HGRN_HEAD), F32)],
        compiler_params=_params("parallel", "arbitrary"),
        name="hgrn_mix",
    )(q, f, i, og, lb.reshape(1, w), ng.reshape(1, HGRN_HEAD))


def _mix_out_kernel(om_ref, mq_ref, mkv_ref, w_ref, h_ref, g_ref, out_ref):
    om = om_ref[0]
    mq = mq_ref[0]
    mem_w = mq.shape[1]
    pairs = []
    for p in range(mem_w // LANES):
        ps = slice(p * LANES, (p + 1) * LANES)
        qp = mq[:, ps]
        kp = mkv_ref[0, :, ps]
        vp = mkv_ref[0, :, mem_w + p * LANES:mem_w + (p + 1) * LANES]
        o_pair = None
        for hh in range(LANES // HEAD_DIM):
            m = _head_mask(hh)
            qm = jnp.where(m, qp, jnp.zeros_like(qp))
            s = lax.dot_general(qm, kp, _NT, preferred_element_type=F32) * (HEAD_DIM ** -0.5)
            e = jnp.exp(s - jnp.max(s, axis=-1, keepdims=True))
            o = jnp.dot(e.astype(BF16), vp, preferred_element_type=F32) / jnp.sum(e, axis=-1, keepdims=True)
            o_pair = o if o_pair is None else jnp.where(m, o, o_pair)
        pairs.append(o_pair.astype(BF16))
    o_mem = jnp.concatenate(pairs, axis=1)
    main_w = om.shape[1]
    mixed = (jnp.dot(om, w_ref[0:main_w, :], preferred_element_type=F32)
             + jnp.dot(o_mem, w_ref[main_w:, :], preferred_element_type=F32))
    out_ref[0] = h_ref[0] + _rms(mixed, g_ref[...])


def mix_out(o_main, mq, mkv, w_out, h, g_post, tm):
    bsz, t, d = h.shape
    tm = min(tm, t)
    assert t % tm == 0
    mw, mm = o_main.shape[2], mq.shape[2]
    return pl.pallas_call(
        _mix_out_kernel,
        out_shape=jax.ShapeDtypeStruct((bsz, t, d), F32),
        grid=(bsz, t // tm),
        in_specs=[pl.BlockSpec((1, tm, mw), lambda b, s: (b, s, 0)),
                  pl.BlockSpec((1, tm, mm), lambda b, s: (b, s, 0)),
                  pl.BlockSpec((1,) + mkv.shape[1:], lambda b, s: (b, 0, 0)),
                  _resident(w_out.shape),
                  pl.BlockSpec((1, tm, d), lambda b, s: (b, s, 0)),
                  _resident((1, d))],
        out_specs=pl.BlockSpec((1, tm, d), lambda b, s: (b, s, 0)),
        compiler_params=_params("parallel", "parallel"),
        name="mix_out",
    )(o_main, mq, mkv, w_out, h, g_post.reshape(1, d))


def _dense_ffn_kernel(h_ref, gpre_ref, wgu_ref, wd_ref, gpost_ref, out_ref, acc_ref, *, dff, chunk):
    h = h_ref[...]
    u = _rms(h, gpre_ref[...]).astype(BF16)
    for idx, c in enumerate(range(0, dff, chunk)):
        g = jnp.dot(u, wgu_ref[:, c:c + chunk], preferred_element_type=F32)
        up = jnp.dot(u, wgu_ref[:, dff + c:dff + c + chunk], preferred_element_type=F32)
        a = (_silu(g) * up).astype(BF16)
        d = jnp.dot(a, wd_ref[c:c + chunk, :], preferred_element_type=F32)
        if idx == 0:
            acc_ref[...] = d
        else:
            acc_ref[...] += d
    out_ref[...] = h + _rms(acc_ref[...], gpost_ref[...])


def dense_ffn(h, g_pre, w_gu, w_d, g_post, tm, chunk=256):
    n, d = h.shape
    dff = w_d.shape[0]
    tm = min(tm, n)
    assert n % tm == 0 and dff % chunk == 0
    return pl.pallas_call(
        functools.partial(_dense_ffn_kernel, dff=dff, chunk=chunk),
        out_shape=jax.ShapeDtypeStruct((n, d), F32),
        grid=(n // tm,),
        in_specs=[pl.BlockSpec((tm, d), lambda i: (i, 0)),
                  _resident((1, d)), _resident(w_gu.shape), _resident(w_d.shape), _resident((1, d))],
        out_specs=pl.BlockSpec((tm, d), lambda i: (i, 0)),
        scratch_shapes=[pltpu.VMEM((tm, d), F32)],
        compiler_params=_params("parallel"),
        name="dense_ffn",
    )(h, g_pre.reshape(1, d), w_gu, w_d, g_post.reshape(1, d))


def _sb_kernel(q_ref, k_ref, v_ref, o_ref, *, tb):
    i = pl.program_id(2)
    row = lax.broadcasted_iota(I32, (tb, tb), 0)
    col = lax.broadcasted_iota(I32, (tb, tb), 1)
    strict = col < row
    after = (row > col).astype(BF16)
    q = q_ref[0]
    out = None
    for hh in range(LANES // HEAD_DIM):
        m = _head_mask(hh)
        qm = jnp.where(m, q.astype(F32) * (HEAD_DIM ** -0.5), 0.0).astype(BF16)

        def block(jj, run, acc, diag, qm=qm):
            ks = pl.ds(pl.multiple_of(jj * tb, tb), tb)
            z = lax.dot_general(qm, k_ref[0, ks, :], _NT, preferred_element_type=F32)
            sp = jnp.maximum(z, 0.0) + jnp.log(1.0 + jnp.exp(-jnp.abs(z)))
            log_keep = -sp
            if diag:
                log_keep = jnp.where(strict, log_keep, 0.0)
            hi = log_keep.astype(BF16)
            lo = (log_keep - hi.astype(F32)).astype(BF16)
            later = (jnp.dot(hi, after, preferred_element_type=F32)
                     + jnp.dot(lo, after, preferred_element_type=F32) + run)
            w = jnp.exp(z - sp + later)
            if diag:
                w = jnp.where(strict, w, 0.0)
            acc = acc + jnp.dot(w.astype(BF16), v_ref[0, ks, :], preferred_element_type=F32)
            run = run + jnp.sum(log_keep, axis=-1, keepdims=True)
            return run, acc

        run, acc = block(i, jnp.zeros((tb, 1), F32), jnp.zeros((tb, LANES), F32), True)
        run, acc = lax.fori_loop(0, i, lambda s, c: block(i - 1 - s, c[0], c[1], False), (run, acc))
        out = acc if out is None else jnp.where(m, acc, out)
    o_ref[0] = out.astype(o_ref.dtype)


def sb_attention(q, kv, tb):
    bsz, t, w = q.shape
    tb = min(tb, t)
    npair = w // LANES
    assert t % tb == 0
    return pl.pallas_call(
        functools.partial(_sb_kernel, tb=tb),
        out_shape=jax.ShapeDtypeStruct((bsz, t, w), BF16),
        grid=(bsz, npair, t // tb),
        in_specs=[pl.BlockSpec((1, tb, LANES), lambda b, p, i: (b, i, p)),
                  pl.BlockSpec((1, t, LANES), lambda b, p, i: (b, 0, p)),
                  pl.BlockSpec((1, t, LANES), lambda b, p, i: (b, 0, npair + p))],
        out_specs=pl.BlockSpec((1, tb, LANES), lambda b, p, i: (b, i, p)),
        compiler_params=_params("parallel", "parallel", "arbitrary"),
        name="sb_attention",
    )(q, kv, kv)


def _router_kernel(h_ref, g_ref, wrt_ref, before_ref, upk_ref, idx_ref, rank_ref, gate_ref, cnt_ref, run_ref):
    @pl.when(pl.program_id(0) == 0)
    def _():
        run_ref[...] = jnp.zeros_like(run_ref)

    u = _rms(h_ref[...], g_ref[...])
    tm, d = u.shape
    ne = wrt_ref.shape[0]
    logits = lax.dot_general(wrt_ref[...], u, _NT, precision=lax.Precision.HIGHEST,
                             preferred_element_type=F32)
    eio = lax.broadcasted_iota(I32, (ne, tm), 0)
    m1 = jnp.max(logits, axis=0, keepdims=True)
    i1 = jnp.min(jnp.where(logits == m1, eio, ne), axis=0, keepdims=True)
    rest = jnp.where(eio == i1, -jnp.inf, logits)
    m2 = jnp.max(rest, axis=0, keepdims=True)
    i2 = jnp.min(jnp.where(rest == m2, eio, ne), axis=0, keepdims=True)
    e21 = jnp.exp(m2 - m1)
    g1 = 1.0 / (1.0 + e21)
    gate_ref[0:1, :] = g1
    gate_ref[1:2, :] = e21 * g1
    oh1 = eio == i1
    oh2 = eio == i2
    cnt = jnp.where(oh1 | oh2, 1.0, 0.0)
    base = jnp.dot(cnt.astype(BF16), before_ref[...], preferred_element_type=F32) + run_ref[:, 0:1]
    idx_ref[0:1, :] = i1
    idx_ref[1:2, :] = i2
    rank_ref[0:1, :] = jnp.sum(jnp.where(oh1, base, 0.0), axis=0, keepdims=True).astype(I32)
    rank_ref[1:2, :] = jnp.sum(jnp.where(oh2, base, 0.0), axis=0, keepdims=True).astype(I32)
    run_ref[...] += jnp.broadcast_to(jnp.sum(cnt, axis=1, keepdims=True), run_ref.shape)
    cnt_ref[...] = run_ref[...]
    bits = lax.bitcast_convert_type(u.astype(BF16).astype(F32), U32)
    upk_ref[...] = (bits[:, :d // 2] & jnp.uint32(0xFFFF0000)) | (bits[:, d // 2:] >> 16)


def _zero_fill_copies(zero_ref, xbuf_ref, sem, first, length):
    copies = []
    for r in range(SUBLANES - 1):
        copies.append((r < length % SUBLANES,
                       pltpu.make_async_copy(zero_ref.at[pl.ds(r, 1)], xbuf_ref.at[pl.ds(first + r, 1)], sem)))
    pos = first + length % SUBLANES
    size = SUBLANES
    while size < MOE_ROWS:
        start = pl.multiple_of(pos, SUBLANES)
        copies.append(((length & size) != 0,
                       pltpu.make_async_copy(zero_ref.at[pl.ds(0, size)], xbuf_ref.at[pl.ds(start, size)], sem)))
        pos = pos + (length & size)
        size *= 2
    return copies


def _dispatch_kernel(tail_sm, pad_sm, nb_sm, dest_sm, u_ref, xbuf_ref, zero_ref, sem, *, tm, ne, min_blocks, max_blocks):
    def issue(t, carry):
        for k in range(2):
            pltpu.make_async_copy(u_ref.at[pl.ds(t, 1)], xbuf_ref.at[pl.ds(dest_sm[k, t], 1)], sem).start()
        return carry

    lax.fori_loop(0, tm, issue, 0, unroll=8)
    for k in range(2):
        pltpu.make_async_copy(u_ref, xbuf_ref.at[pl.ds(0, tm)], sem).wait()

    @pl.when(pl.program_id(0) == pl.num_programs(0) - 1)
    def _():
        zero_ref[...] = jnp.zeros_like(zero_ref)
        copies = []
        for e in range(ne):
            copies += _zero_fill_copies(zero_ref, xbuf_ref, sem, tail_sm[e], pad_sm[e])
        for b in range(min_blocks, max_blocks):
            copies.append((b >= nb_sm[0],
                           pltpu.make_async_copy(zero_ref, xbuf_ref.at[pl.ds(b * MOE_ROWS, MOE_ROWS)], sem)))
        for cond, cp in copies:
            pl.when(cond)(cp.start)
        for cond, cp in copies:
            pl.when(cond)(cp.wait)


def _experts_kernel(be_sm, nb_sm, x_ref, wg_ref, wu_ref, wd_ref, y_ref, xb_ref, acc_ref, *, tf, chunk):
    b = pl.program_id(0)
    j = pl.program_id(1)
    nj = pl.num_programs(1)

    @pl.when(b < nb_sm[0])
    def _():
        @pl.when(j == 0)
        def _():
            words = x_ref[...]
            half = words.shape[1]
            xb_ref[:, :half] = lax.bitcast_convert_type(words & jnp.uint32(0xFFFF0000), F32).astype(BF16)
            xb_ref[:, half:] = lax.bitcast_convert_type(words << 16, F32).astype(BF16)
            acc_ref[...] = jnp.zeros_like(acc_ref)

        x = xb_ref[...]
        for c in range(0, tf, chunk):
            g = jnp.dot(x, wg_ref[0, :, c:c + chunk], preferred_element_type=F32)
            up = jnp.dot(x, wu_ref[0, :, c:c + chunk], preferred_element_type=F32)
            a = (_silu(g) * up).astype(BF16)
            acc_ref[...] += jnp.dot(a, wd_ref[0, c:c + chunk, :], preferred_element_type=F32)

        @pl.when(j == nj - 1)
        def _():
            y_ref[...] = acc_ref[...]

    @pl.when((b >= nb_sm[0]) & (j == nj - 1))
    def _():
        y_ref[...] = jnp.zeros_like(y_ref)


def _combine_kernel(dest_sm, gate_ref, h_ref, g_ref, ybuf_ref, out_ref, y0_ref, y1_ref, sem, *, tm):
    def issue(t, carry):
        pltpu.make_async_copy(ybuf_ref.at[pl.ds(dest_sm[0, t], 1)], y0_ref.at[pl.ds(t, 1)], sem).start()
        pltpu.make_async_copy(ybuf_ref.at[pl.ds(dest_sm[1, t], 1)], y1_ref.at[pl.ds(t, 1)], sem).start()
        return carry

    lax.fori_loop(0, tm, issue, 0, unroll=8)
    pltpu.make_async_copy(ybuf_ref.at[pl.ds(0, tm)], y0_ref, sem).wait()
    pltpu.make_async_copy(ybuf_ref.at[pl.ds(0, tm)], y1_ref, sem).wait()
    gate = gate_ref[...]
    y = gate[:, 0:1] * y0_ref[...] + gate[:, 1:2] * y1_ref[...]
    out_ref[...] = h_ref[...] + _rms(y, g_ref[...])


def moe_ffn(h, g_pre, w_router, w_gu, w_d, g_post, tm, tf):
    n, d = h.shape
    ne = w_router.shape[1]
    dff = w_d.shape[1]
    tm = min(tm, n)
    half = d // 2
    rows = MOE_ROWS
    assert n % tm == 0 and n % rows == 0 and dff % tf == 0
    min_blocks = 2 * n // rows
    max_blocks = min_blocks + ne

    before = (jnp.arange(tm)[:, None] < jnp.arange(tm)[None, :]).astype(BF16)
    upk, idx, rank, gates, cnt = pl.pallas_call(
        _router_kernel,
        out_shape=[jax.ShapeDtypeStruct((n, half), U32),
                   jax.ShapeDtypeStruct((2, n), I32),
                   jax.ShapeDtypeStruct((2, n), I32),
                   jax.ShapeDtypeStruct((2, n), F32),
                   jax.ShapeDtypeStruct((ne, LANES), F32)],
        grid=(n // tm,),
        in_specs=[pl.BlockSpec((tm, d), lambda i: (i, 0)),
                  _resident((1, d)), _resident((ne, d)), _resident((tm, tm))],
        out_specs=[pl.BlockSpec((tm, half), lambda i: (i, 0)),
                   pl.BlockSpec((2, tm), lambda i: (0, i)),
                   pl.BlockSpec((2, tm), lambda i: (0, i)),
                   pl.BlockSpec((2, tm), lambda i: (0, i)),
                   pl.BlockSpec((ne, LANES), lambda i: (0, 0))],
        scratch_shapes=[pltpu.VMEM((ne, LANES), F32)],
        compiler_params=_params("arbitrary"),
        name="moe_router",
    )(h, g_pre.reshape(1, d), w_router.T, before)

    counts = cnt[:, 0].astype(I32)
    nb_e = (counts + rows - 1) // rows
    ends = jnp.cumsum(nb_e)
    nblk = ends[-1:].astype(I32)
    first_row = (ends - nb_e) * rows
    dest = (first_row[idx] + rank).astype(I32)
    blk_e = jnp.minimum(jnp.searchsorted(ends, jnp.arange(max_blocks, dtype=I32), side="right"), ne - 1).astype(I32)

    x_buf = pl.pallas_call(
        functools.partial(_dispatch_kernel, tm=tm, ne=ne, min_blocks=min_blocks, max_blocks=max_blocks),
        out_shape=jax.ShapeDtypeStruct((max_blocks * rows, half), U32),
        grid_spec=pltpu.PrefetchScalarGridSpec(
            num_scalar_prefetch=3,
            grid=(n // tm,),
            in_specs=[pl.BlockSpec((2, tm), lambda i, *_: (0, i), memory_space=pltpu.SMEM),
                      pl.BlockSpec((tm, half), lambda i, *_: (i, 0))],
            out_specs=pl.BlockSpec(memory_space=pl.ANY),
            scratch_shapes=[pltpu.VMEM((rows, half), U32), pltpu.SemaphoreType.DMA(())]),
        compiler_params=_params("arbitrary"),
        name="moe_dispatch",
    )((first_row + counts).astype(I32), (nb_e * rows - counts).astype(I32), nblk, dest, upk)

    nj = dff // tf

    def live(b, nb):
        return jnp.minimum(b, nb[0] - 1)

    y_buf = pl.pallas_call(
        functools.partial(_experts_kernel, tf=tf, chunk=min(tf, 512)),
        out_shape=jax.ShapeDtypeStruct((max_blocks * rows, d), F32),
        grid_spec=pltpu.PrefetchScalarGridSpec(
            num_scalar_prefetch=2,
            grid=(max_blocks, nj),
            in_specs=[pl.BlockSpec((rows, half), lambda b, j, be, nb: (live(b, nb), 0)),
                      pl.BlockSpec((1, d, tf), lambda b, j, be, nb: (be[live(b, nb)], 0, j)),
                      pl.BlockSpec((1, d, tf), lambda b, j, be, nb: (be[live(b, nb)], 0, nj + j)),
                      pl.BlockSpec((1, tf, d), lambda b, j, be, nb: (be[live(b, nb)], j, 0))],
            out_specs=pl.BlockSpec((rows, d), lambda b, j, be, nb: (b, 0)),
            scratch_shapes=[pltpu.VMEM((rows, d), BF16), pltpu.VMEM((rows, d), F32)]),
        compiler_params=_params("arbitrary", "arbitrary"),
        name="moe_experts",
    )(blk_e, nblk, x_buf, w_gu, w_gu, w_d)

    return pl.pallas_call(
        functools.partial(_combine_kernel, tm=tm),
        out_shape=jax.ShapeDtypeStruct((n, d), F32),
        grid=(n // tm,),
        in_specs=[pl.BlockSpec((2, tm), lambda i: (0, i), memory_space=pltpu.SMEM),
                  pl.BlockSpec((tm, 2), lambda i: (i, 0)),
                  pl.BlockSpec((tm, d), lambda i: (i, 0)),
                  _resident((1, d)),
                  pl.BlockSpec(memory_space=pl.ANY)],
        out_specs=pl.BlockSpec((tm, d), lambda i: (i, 0)),
        scratch_shapes=[pltpu.VMEM((tm, d), F32), pltpu.VMEM((tm, d), F32), pltpu.SemaphoreType.DMA(())],
        compiler_params=_params("arbitrary"),
        name="moe_combine",
    )(dest, gates.T, h, g_post.reshape(1, d), y_buf)


def kernel(x, mem, attn_norm_pre, attn_norm_post, ffn_norm_pre, ffn_norm_post, w_in_a, hgrn_lower_bounds, hgrn_out_norm, w_in_b, kv_norm, w_kv, mem_norm, w_mem_kv, w_out, w_ffn_gate_up, w_ffn_down, w_router, w_exp_gate_up, w_exp_down):
    bsz, t, d = x.shape
    n = bsz * t
    m = mem.shape[1]
    depth = attn_norm_pre.shape[0]
    n_a = w_in_a.shape[0]
    hgrn_w = hgrn_lower_bounds.shape[1]
    mem_w = w_mem_kv.shape[2] // 2
    sb_w = w_kv.shape[1] // 2
    lower_bounds = jnp.cumsum(jax.nn.softmax(hgrn_lower_bounds.astype(F32), axis=0), axis=0)

    h = x.reshape(n, d)
    mem2 = mem.reshape(bsz * m, d)
    sb_kv = None
    for layer in range(depth):
        (mkv,) = norm_matmul(mem2, mem_norm[layer], w_mem_kv[layer].astype(BF16),
                             [2 * mem_w], [BF16], 512, "mem_kv_proj")
        mkv = mkv.reshape(bsz, m, 2 * mem_w)
        if layer < n_a:
            q, f, i, og, mq = norm_matmul(
                h, attn_norm_pre[layer], w_in_a[layer].astype(BF16),
                [hgrn_w, hgrn_w, hgrn_w, hgrn_w, mem_w], [BF16, F32, BF16, BF16, BF16], 512, "in_proj_a")
            r3 = lambda a: a.reshape(bsz, t, a.shape[-1])
            o_main = hgrn_mix(r3(q), r3(f), r3(i), r3(og), lower_bounds[layer], hgrn_out_norm[layer], 256)
        else:
            sq, mq = norm_matmul(h, attn_norm_pre[layer], w_in_b[layer - n_a].astype(BF16),
                                 [sb_w, mem_w], [BF16, BF16], 512, "in_proj_b")
            o_main = sb_attention(sq.reshape(bsz, t, sb_w), sb_kv, 256)
        h = mix_out(o_main, mq.reshape(bsz, t, mem_w), mkv, w_out[layer].astype(BF16),
                    h.reshape(bsz, t, d), attn_norm_post[layer], 512).reshape(n, d)
        if layer % 2 == 0:
            h = dense_ffn(h, ffn_norm_pre[layer], w_ffn_gate_up[layer // 2].astype(BF16),
                          w_ffn_down[layer // 2].astype(BF16), ffn_norm_post[layer], 512)
        else:
            h = moe_ffn(h, ffn_norm_pre[layer], w_router[layer // 2], w_exp_gate_up[layer // 2].astype(BF16),
                        w_exp_down[layer // 2].astype(BF16), ffn_norm_post[layer], 512, 512)
        if layer == n_a - 1:
            (sb_kv,) = norm_matmul(h, kv_norm, w_kv.astype(BF16), [2 * sb_w], [BF16], 512, "kv_proj")
            sb_kv = sb_kv.reshape(bsz, t, 2 * sb_w)
    return h.reshape(bsz, t, d)
```

```python
import functools

import jax
import jax.numpy as jnp
from jax import lax
from jax.experimental import pallas as pl
from jax.experimental.pallas import tpu as pltpu

F32 = jnp.float32
BF16 = jnp.bfloat16
I32 = jnp.int32
U32 = jnp.uint32

RMS_EPS = 1e-6
LANES = 128
SUBLANES = 8
HEAD_DIM = 64
HGRN_HEAD = 128
HGRN_CHUNK = 64
MOE_ROWS = 512
LOG2_E = 1.4426950408889634
SB_UNDERFLOW = 151.0
VMEM_LIMIT = 56 * 1024 * 1024

_NT = (((1,), (1,)), ((), ()))


def _params(*sem):
    return pltpu.CompilerParams(dimension_semantics=sem, vmem_limit_bytes=VMEM_LIMIT)


def _resident(shape):
    zeros = (0,) * len(shape)
    return pl.BlockSpec(shape, lambda *_: zeros, pipeline_mode=pl.Buffered(1))


def _rms(xf, g):
    var = jnp.mean(xf * xf, axis=-1, keepdims=True)
    return xf * lax.rsqrt(var + RMS_EPS) * g


def _silu(x):
    return x * jax.nn.sigmoid(x)


def _head_mask(hh):
    lane = lax.broadcasted_iota(I32, (1, LANES), 1)
    return (lane >= hh * HEAD_DIM) & (lane < (hh + 1) * HEAD_DIM)


def _norm_matmul_kernel(x_ref, g_ref, w_ref, *out_refs, splits):
    xn = _rms(x_ref[...], g_ref[...]).astype(BF16)
    off = 0
    for o_ref, width in zip(out_refs, splits):
        for c in range(0, width, 512):
            cw = min(512, width - c)
            o_ref[:, c:c + cw] = jnp.dot(
                xn, w_ref[:, off + c:off + c + cw], preferred_element_type=F32).astype(o_ref.dtype)
        off += width


def norm_matmul(x, g, w, splits, out_dtypes, tm, name):
    n, d = x.shape
    tm = min(tm, n)
    assert n % tm == 0 and sum(splits) == w.shape[1]
    return pl.pallas_call(
        functools.partial(_norm_matmul_kernel, splits=tuple(splits)),
        out_shape=[jax.ShapeDtypeStruct((n, s), dt) for s, dt in zip(splits, out_dtypes)],
        grid=(n // tm,),
        in_specs=[pl.BlockSpec((tm, d), lambda i: (i, 0)),
                  _resident((1, d)),
                  _resident(w.shape)],
        out_specs=[pl.BlockSpec((tm, s), lambda i: (i, 0)) for s in splits],
        compiler_params=_params("parallel"),
        name=name,
    )(x, g.reshape(1, d), w)


def _hgrn_kernel(q_ref, f_ref, i_ref, og_ref, lb_ref, ng_ref, o_ref, st_ref, *, n_chunks, heads):
    C, dh = HGRN_CHUNK, HGRN_HEAD

    @pl.when(pl.program_id(1) == 0)
    def _():
        st_ref[...] = jnp.zeros_like(st_ref)

    lb = lb_ref[...]
    ng = ng_ref[...]
    row = lax.broadcasted_iota(I32, (C, C), 0)
    col = lax.broadcasted_iota(I32, (C, C), 1)
    causal = row >= col
    tri = causal.astype(BF16)

    for c in range(n_chunks):
        sl = pl.ds(c * C, C)
        fg = lb + (1.0 - lb) * jax.nn.sigmoid(f_ref[0, sl, :])
        kh = 1.0 - fg
        gh = jnp.log(fg)
        g1 = gh.astype(BF16)
        g2 = (gh - g1.astype(F32)).astype(BF16)
        b = jnp.dot(tri, g1, preferred_element_type=F32) + jnp.dot(tri, g2, preferred_element_type=F32)
        b_last = b[C - 1:C, :]
        q_dec = (_silu(q_ref[0, sl, :].astype(F32)) * jnp.exp(b)).astype(BF16)
        k_intra = (kh * jnp.exp(-b)).astype(BF16)
        k_state = (kh * jnp.exp(b_last - b)).astype(BF16)
        dec = jnp.exp(b_last)
        v = i_ref[0, sl, :].astype(F32)
        og = og_ref[0, sl, :].astype(F32)
        for h in range(heads):
            hs = slice(h * dh, (h + 1) * dh)
            qd_h = q_dec[:, hs]
            v_h = v[:, hs]
            v_hb = v_h.astype(BF16)
            scores = lax.dot_general(qd_h, k_intra[:, hs], _NT, preferred_element_type=F32)
            scores = jnp.where(causal, scores, 0.0).astype(BF16)
            st = st_ref[h]
            o = (jnp.dot(scores, v_hb, preferred_element_type=F32)
                 + lax.dot_general(qd_h, st.astype(BF16), _NT, preferred_element_type=F32))
            st_ref[h] = st * dec[:, hs] + jnp.dot(
                v_h.T.astype(BF16), k_state[:, hs], preferred_element_type=F32)
            o = _rms(o, ng) * _silu(og[:, hs])
            o_ref[0, sl, hs] = o.astype(o_ref.dtype)


def hgrn_mix(q, f, i, og, lb, ng, tt):
    bsz, t, w = q.shape
    heads = w // HGRN_HEAD
    tt = min(tt, t)
    assert t % tt == 0 and tt % HGRN_CHUNK == 0
    blk = pl.BlockSpec((1, tt, w), lambda b, s: (b, s, 0))
    return pl.pallas_call(
        functools.partial(_hgrn_kernel, n_chunks=tt // HGRN_CHUNK, heads=heads),
        out_shape=jax.ShapeDtypeStruct((bsz, t, w), BF16),
        grid=(bsz, t // tt),
        in_specs=[blk, blk, blk, blk, _resident((1, w)), _resident((1, HGRN_HEAD))],
        out_specs=blk,
        scratch_shapes=[pltpu.VMEM((heads, HGRN_HEAD, HGRN_HEAD), F32)],
        compiler_params=_params("parallel", "arbitrary"),
        name="hgrn_mix",
    )(q, f, i, og, lb.reshape(1, w), ng.reshape(1, HGRN_HEAD))


def _mix_out_kernel(om_ref, mq_ref, mkv_ref, w_ref, h_ref, g_ref, out_ref):
    om = om_ref[0]
    mq = mq_ref[0]
    mem_w = mq.shape[1]
    pairs = []
    for p in range(mem_w // LANES):
        ps = slice(p * LANES, (p + 1) * LANES)
        qp = mq[:, ps]
        kp = mkv_ref[0, :, ps]
        vp = mkv_ref[0, :, mem_w + p * LANES:mem_w + (p + 1) * LANES]
        o_pair = None
        for hh in range(LANES // HEAD_DIM):
            m = _head_mask(hh)
            qm = jnp.where(m, qp, jnp.zeros_like(qp))
            s = lax.dot_general(qm, kp, _NT, preferred_element_type=F32) * (HEAD_DIM ** -0.5)
            e = jnp.exp(s - jnp.max(s, axis=-1, keepdims=True))
            o = jnp.dot(e.astype(BF16), vp, preferred_element_type=F32) / jnp.sum(e, axis=-1, keepdims=True)
            o_pair = o if o_pair is None else jnp.where(m, o, o_pair)
        pairs.append(o_pair.astype(BF16))
    o_mem = jnp.concatenate(pairs, axis=1)
    main_w = om.shape[1]
    mixed = (jnp.dot(om, w_ref[0:main_w, :], preferred_element_type=F32)
             + jnp.dot(o_mem, w_ref[main_w:, :], preferred_element_type=F32))
    out_ref[0] = h_ref[0] + _rms(mixed, g_ref[...])


def mix_out(o_main, mq, mkv, w_out, h, g_post, tm):
    bsz, t, d = h.shape
    tm = min(tm, t)
    assert t % tm == 0
    mw, mm = o_main.shape[2], mq.shape[2]
    return pl.pallas_call(
        _mix_out_kernel,
        out_shape=jax.ShapeDtypeStruct((bsz, t, d), F32),
        grid=(bsz, t // tm),
        in_specs=[pl.BlockSpec((1, tm, mw), lambda b, s: (b, s, 0)),
                  pl.BlockSpec((1, tm, mm), lambda b, s: (b, s, 0)),
                  pl.BlockSpec((1,) + mkv.shape[1:], lambda b, s: (b, 0, 0)),
                  _resident(w_out.shape),
                  pl.BlockSpec((1, tm, d), lambda b, s: (b, s, 0)),
                  _resident((1, d))],
        out_specs=pl.BlockSpec((1, tm, d), lambda b, s: (b, s, 0)),
        compiler_params=_params("parallel", "parallel"),
        name="mix_out",
    )(o_main, mq, mkv, w_out, h, g_post.reshape(1, d))


def _dense_ffn_kernel(h_ref, gpre_ref, wgu_ref, wd_ref, gpost_ref, out_ref, acc_ref, *, dff, chunk):
    h = h_ref[...]
    u = _rms(h, gpre_ref[...]).astype(BF16)
    for idx, c in enumerate(range(0, dff, chunk)):
        g = jnp.dot(u, wgu_ref[:, c:c + chunk], preferred_element_type=F32)
        up = jnp.dot(u, wgu_ref[:, dff + c:dff + c + chunk], preferred_element_type=F32)
        a = (_silu(g) * up).astype(BF16)
        d = jnp.dot(a, wd_ref[c:c + chunk, :], preferred_element_type=F32)
        if idx == 0:
            acc_ref[...] = d
        else:
            acc_ref[...] += d
    out_ref[...] = h + _rms(acc_ref[...], gpost_ref[...])


def dense_ffn(h, g_pre, w_gu, w_d, g_post, tm, chunk=256):
    n, d = h.shape
    dff = w_d.shape[0]
    tm = min(tm, n)
    assert n % tm == 0 and dff % chunk == 0
    return pl.pallas_call(
        functools.partial(_dense_ffn_kernel, dff=dff, chunk=chunk),
        out_shape=jax.ShapeDtypeStruct((n, d), F32),
        grid=(n // tm,),
        in_specs=[pl.BlockSpec((tm, d), lambda i: (i, 0)),
                  _resident((1, d)), _resident(w_gu.shape), _resident(w_d.shape), _resident((1, d))],
        out_specs=pl.BlockSpec((tm, d), lambda i: (i, 0)),
        scratch_shapes=[pltpu.VMEM((tm, d), F32)],
        compiler_params=_params("parallel"),
        name="dense_ffn",
    )(h, g_pre.reshape(1, d), w_gu, w_d, g_post.reshape(1, d))


def _sb_kernel(q_ref, k_ref, v_ref, o_ref, *, tb, together):
    nh = LANES // HEAD_DIM
    row = lax.broadcasted_iota(I32, (tb, tb), 0)
    col = lax.broadcasted_iota(I32, (tb, tb), 1)
    from_here = (row >= col).astype(BF16)
    before = jnp.concatenate([col < row] * nh, axis=0)
    masks = [_head_mask(hh) for hh in range(nh)]

    def rows_of(i):
        return pl.ds(pl.multiple_of(i * tb, tb), tb)

    def block(qm, jj, spent, acc, diag):
        z2 = lax.dot_general(qm, k_ref[0, rows_of(jj), :], _NT, preferred_element_type=F32) * LOG2_E
        neg_abs = lax.bitcast_convert_type(lax.bitcast_convert_type(z2, U32) | jnp.uint32(0x80000000), F32)
        cost = jnp.maximum(z2, 0.0) + jnp.log2(1.0 + jnp.exp2(neg_abs))
        if diag:
            cost = jnp.where(before, cost, 0.0)
        spent_new = spent + jnp.sum(cost, axis=-1, keepdims=True)
        alive = (jnp.min(spent_new) < SB_UNDERFLOW).astype(I32)
        tail = jnp.dot(cost.astype(BF16), from_here, preferred_element_type=F32)
        w = jnp.exp2(z2 - tail - spent)
        if diag:
            w = jnp.where(before, w, 0.0)
        acc = acc + jnp.dot(w.astype(BF16), v_ref[0, rows_of(jj), :], preferred_element_type=F32)
        return spent_new, acc, alive

    def first_blocks(i, has_previous):
        q = q_ref[0, rows_of(i), :].astype(F32) * (HEAD_DIM ** -0.5)
        qm = jnp.concatenate([jnp.where(m, q, 0.0) for m in masks], axis=0).astype(BF16)
        state = block(qm, i, jnp.zeros((nh * tb, 1), F32), jnp.zeros((nh * tb, LANES), F32), True)
        if has_previous:
            state = block(qm, i - 1, state[0], state[1], False)
        return qm, state

    def finish(i, rest, qm, state):
        def body(c):
            return (c[0] - 1,) + block(qm, c[0], c[1], c[2], False)

        _, _, acc, _ = lax.while_loop(lambda c: (c[0] >= 0) & (c[3] > 0), body, (rest,) + state)
        out = acc[0:tb]
        for hh in range(1, nh):
            out = jnp.where(masks[hh], acc[hh * tb:(hh + 1) * tb], out)
        o_ref[0, rows_of(i), :] = out.astype(o_ref.dtype)

    def tiles(first, count, has_previous):
        started = [first_blocks(first + t, has_previous) for t in range(count)]
        for t, (qm, state) in enumerate(started):
            finish(first + t, first + t - (2 if has_previous else 1), qm, state)

    nt = q_ref.shape[1] // tb
    tiles(0, 1, False)
    groups = (nt - 1) // together

    def group(g, carry):
        tiles(1 + g * together, together, True)
        return carry

    lax.fori_loop(0, groups, group, 0)
    for i in range(1 + groups * together, nt):
        tiles(i, 1, True)


def sb_attention(q, kv, tb, together=5):
    bsz, t, w = q.shape
    tb = min(tb, t)
    npair = w // LANES
    assert t % tb == 0
    return pl.pallas_call(
        functools.partial(_sb_kernel, tb=tb, together=together),
        out_shape=jax.ShapeDtypeStruct((bsz, t, w), BF16),
        grid=(bsz, npair),
        in_specs=[pl.BlockSpec((1, t, LANES), lambda b, p: (b, 0, p)),
                  pl.BlockSpec((1, t, LANES), lambda b, p: (b, 0, p)),
                  pl.BlockSpec((1, t, LANES), lambda b, p: (b, 0, npair + p))],
        out_specs=pl.BlockSpec((1, t, LANES), lambda b, p: (b, 0, p)),
        compiler_params=_params("parallel", "parallel"),
        name="sb_attention",
    )(q, kv, kv)


def _router_kernel(h_ref, g_ref, wrt_ref, before_ref, u_ref, idx_ref, rank_ref, gate_ref, cnt_ref, run_ref):
    @pl.when(pl.program_id(0) == 0)
    def _():
        run_ref[...] = jnp.zeros_like(run_ref)

    u = _rms(h_ref[...], g_ref[...])
    tm = u.shape[0]
    ne = wrt_ref.shape[0]
    logits = lax.dot_general(wrt_ref[...], u, _NT, precision=lax.Precision.HIGHEST,
                             preferred_element_type=F32)
    eio = lax.broadcasted_iota(I32, (ne, tm), 0)
    m1 = jnp.max(logits, axis=0, keepdims=True)
    i1 = jnp.min(jnp.where(logits == m1, eio, ne), axis=0, keepdims=True)
    rest = jnp.where(eio == i1, -jnp.inf, logits)
    m2 = jnp.max(rest, axis=0, keepdims=True)
    i2 = jnp.min(jnp.where(rest == m2, eio, ne), axis=0, keepdims=True)
    e21 = jnp.exp(m2 - m1)
    g1 = 1.0 / (1.0 + e21)
    gate_ref[0:1, :] = g1
    gate_ref[1:2, :] = e21 * g1
    oh1 = eio == i1
    oh2 = eio == i2
    cnt = jnp.where(oh1 | oh2, 1.0, 0.0)
    base = jnp.dot(cnt.astype(BF16), before_ref[...], preferred_element_type=F32) + run_ref[:, 0:1]
    idx_ref[0:1, :] = i1
    idx_ref[1:2, :] = i2
    rank_ref[0:1, :] = jnp.sum(jnp.where(oh1, base, 0.0), axis=0, keepdims=True).astype(I32)
    rank_ref[1:2, :] = jnp.sum(jnp.where(oh2, base, 0.0), axis=0, keepdims=True).astype(I32)
    run_ref[...] += jnp.broadcast_to(jnp.sum(cnt, axis=1, keepdims=True), run_ref.shape)
    cnt_ref[...] = run_ref[...]
    u_ref[...] = u


def _invert_kernel(tail_sm, pad_sm, nb_sm, dest_sm, src_sm, *, tm, n, ne):
    i = pl.program_id(0)

    @pl.when(i == 0)
    def _():
        def clear(r, carry):
            src_sm[r] = 2 * n
            return carry

        for e in range(ne):
            lax.fori_loop(tail_sm[e], tail_sm[e] + pad_sm[e], clear, 0)
        lax.fori_loop(nb_sm[0] * MOE_ROWS, src_sm.shape[0], clear, 0)

    def record(t, carry):
        for k in range(2):
            src_sm[dest_sm[2 * t + k]] = k * n + i * tm + t
        return carry

    lax.fori_loop(0, tm, record, 0, unroll=8)


def _experts_kernel(be_sm, nb_sm, prev_sm, next_sm, u_hbm, wgu_hbm, wd_hbm, y_hbm,
                    xg_ref, yv_ref, xb_ref, acc_ref, wgu_ref, wd_ref, wsems, gsem, ssem, *, n, chunk):
    b = pl.program_id(0)
    nb = nb_sm[0]
    live = b < nb
    rows, d = xg_ref.shape[1:]
    dff = wd_ref.shape[0]
    slot = b % 2
    other = 1 - slot

    def gather(src, r, into):
        token = jnp.where(src >= n, src - n, src)
        token = jnp.minimum(token, n - 1)
        return pltpu.make_async_copy(u_hbm.at[pl.ds(token, 1)], xg_ref.at[into, pl.ds(r, 1)], gsem)

    def scatter(src, r, out_of, spare):
        row = jnp.where(spare | (src >= 2 * n), 2 * n + r, src)
        return pltpu.make_async_copy(yv_ref.at[out_of, pl.ds(r, 1)], y_hbm.at[pl.ds(row, 1)], ssem)

    def wait_rows(sem):
        pltpu.make_async_copy(u_hbm.at[pl.ds(0, rows)], xg_ref.at[0], sem).wait()

    @pl.when(b == 0)
    def _():
        yv_ref[1] = jnp.zeros(yv_ref.shape[1:], yv_ref.dtype)

        def first(r, carry):
            gather(prev_sm[r], r, 0).start()
            return carry

        lax.fori_loop(0, rows, first, 0, unroll=8)

    @pl.when(b <= nb)
    def _():
        wait_rows(gsem)

    e = be_sm[jnp.minimum(b, nb - 1)]

    @pl.when(live & ((b == 0) | (e != be_sm[jnp.maximum(b - 1, 0)])))
    def _():
        fetch = [pltpu.make_async_copy(wgu_hbm.at[e], wgu_ref, wsems.at[0]),
                 pltpu.make_async_copy(wd_hbm.at[e], wd_ref, wsems.at[1])]
        for cp in fetch:
            cp.start()
        for cp in fetch:
            cp.wait()

    @pl.when(live)
    def _():
        xb_ref[...] = xg_ref[slot].astype(BF16)
        x = xb_ref[...]
        chunks = list(range(0, dff, chunk))
        per_chunk = -(-rows // len(chunks))
        first_step = b == 0
        for ci, c in enumerate(chunks):
            g = jnp.dot(x, wgu_ref[:, c:c + chunk], preferred_element_type=F32)
            up = jnp.dot(x, wgu_ref[:, dff + c:dff + c + chunk], preferred_element_type=F32)
            a = (_silu(g) * up).astype(BF16)
            part = jnp.dot(a, wd_ref[c:c + chunk, :], preferred_element_type=F32)
            if ci == 0:
                acc_ref[...] = part
            else:
                acc_ref[...] += part
            for r in range(ci * per_chunk, min((ci + 1) * per_chunk, rows)):
                gather(next_sm[r], r, other).start(priority=0)
                scatter(prev_sm[r], r, other, first_step).start(priority=1)
        wait_rows(ssem)
        yv_ref[slot] = acc_ref[...]

    @pl.when(b == nb)
    def _():
        def last(r, carry):
            scatter(prev_sm[r], r, other, False).start()
            return carry

        lax.fori_loop(0, rows, last, 0, unroll=8)
        wait_rows(ssem)


def _combine_kernel(y0_ref, y1_ref, gate_ref, h_ref, g_ref, out_ref):
    gate = gate_ref[...]
    y = gate[:, 0:1] * y0_ref[...] + gate[:, 1:2] * y1_ref[...]
    out_ref[...] = h_ref[...] + _rms(y, g_ref[...])


def moe_ffn(h, g_pre, w_router, w_gu, w_d, g_post, tm, chunk=256):
    n, d = h.shape
    ne = w_router.shape[1]
    dff = w_d.shape[1]
    tm = min(tm, n)
    rows = MOE_ROWS
    assert n % tm == 0 and n % rows == 0 and dff % chunk == 0
    max_blocks = 2 * n // rows + ne

    before = (jnp.arange(tm)[:, None] < jnp.arange(tm)[None, :]).astype(BF16)
    u, idx, rank, gates, cnt = pl.pallas_call(
        _router_kernel,
        out_shape=[jax.ShapeDtypeStruct((n, d), F32),
                   jax.ShapeDtypeStruct((2, n), I32),
                   jax.ShapeDtypeStruct((2, n), I32),
                   jax.ShapeDtypeStruct((2, n), F32),
                   jax.ShapeDtypeStruct((ne, LANES), F32)],
        grid=(n // tm,),
        in_specs=[pl.BlockSpec((tm, d), lambda i: (i, 0)),
                  _resident((1, d)), _resident((ne, d)), _resident((tm, tm))],
        out_specs=[pl.BlockSpec((tm, d), lambda i: (i, 0)),
                   pl.BlockSpec((2, tm), lambda i: (0, i)),
                   pl.BlockSpec((2, tm), lambda i: (0, i)),
                   pl.BlockSpec((2, tm), lambda i: (0, i)),
                   pl.BlockSpec((ne, LANES), lambda i: (0, 0))],
        scratch_shapes=[pltpu.VMEM((ne, LANES), F32)],
        compiler_params=_params("arbitrary"),
        name="moe_router",
    )(h, g_pre.reshape(1, d), w_router.T, before)

    counts = cnt[:, 0].astype(I32)
    nb_e = (counts + rows - 1) // rows
    ends = jnp.cumsum(nb_e)
    nblk = ends[-1:].astype(I32)
    first_row = (ends - nb_e) * rows
    experts = jnp.arange(ne, dtype=I32)
    dest = (jnp.sum(jnp.where(idx[None] == experts[:, None, None], first_row[:, None, None], 0), axis=0)
            + rank).astype(I32).T.reshape(2 * n)
    blk_e = jnp.minimum(jnp.sum(jnp.arange(max_blocks, dtype=I32)[:, None] >= ends[None, :], axis=1),
                        ne - 1).astype(I32)

    src = pl.pallas_call(
        functools.partial(_invert_kernel, tm=tm, n=n, ne=ne),
        out_shape=jax.ShapeDtypeStruct((max_blocks * rows,), I32),
        grid_spec=pltpu.PrefetchScalarGridSpec(
            num_scalar_prefetch=3,
            grid=(n // tm,),
            in_specs=[pl.BlockSpec((2 * tm,), lambda i, *_: (i,), memory_space=pltpu.SMEM)],
            out_specs=pl.BlockSpec(memory_space=pltpu.SMEM)),
        compiler_params=_params("arbitrary"),
        name="moe_invert",
    )((first_row + counts).astype(I32), (nb_e * rows - counts).astype(I32), nblk, dest)

    y = pl.pallas_call(
        functools.partial(_experts_kernel, n=n, chunk=chunk),
        out_shape=jax.ShapeDtypeStruct((2 * n + rows, d), F32),
        grid_spec=pltpu.PrefetchScalarGridSpec(
            num_scalar_prefetch=2,
            grid=(max_blocks + 1,),
            in_specs=[pl.BlockSpec((rows,), lambda b, be, nb: (jnp.maximum(b - 1, 0),), memory_space=pltpu.SMEM),
                      pl.BlockSpec((rows,), lambda b, be, nb: (jnp.minimum(b + 1, max_blocks - 1),),
                                   memory_space=pltpu.SMEM),
                      pl.BlockSpec(memory_space=pl.ANY),
                      pl.BlockSpec(memory_space=pl.ANY),
                      pl.BlockSpec(memory_space=pl.ANY)],
            out_specs=pl.BlockSpec(memory_space=pl.ANY),
            scratch_shapes=[pltpu.VMEM((2, rows, d), F32), pltpu.VMEM((2, rows, d), F32),
                            pltpu.VMEM((rows, d), BF16), pltpu.VMEM((rows, d), F32),
                            pltpu.VMEM(w_gu.shape[1:], BF16), pltpu.VMEM(w_d.shape[1:], BF16),
                            pltpu.SemaphoreType.DMA((2,)), pltpu.SemaphoreType.DMA(()),
                            pltpu.SemaphoreType.DMA(())]),
        compiler_params=_params("arbitrary"),
        name="moe_experts",
    )(blk_e, nblk, src, src, u, w_gu, w_d)

    return pl.pallas_call(
        _combine_kernel,
        out_shape=jax.ShapeDtypeStruct((n, d), F32),
        grid=(n // tm,),
        in_specs=[pl.BlockSpec((tm, d), lambda i: (i, 0)),
                  pl.BlockSpec((tm, d), lambda i: (i + n // tm, 0)),
                  pl.BlockSpec((tm, 2), lambda i: (i, 0)),
                  pl.BlockSpec((tm, d), lambda i: (i, 0)),
                  _resident((1, d))],
        out_specs=pl.BlockSpec((tm, d), lambda i: (i, 0)),
        compiler_params=_params("parallel"),
        name="moe_combine",
    )(y, y, gates.T, h, g_post.reshape(1, d))


def kernel(x, mem, attn_norm_pre, attn_norm_post, ffn_norm_pre, ffn_norm_post, w_in_a, hgrn_lower_bounds, hgrn_out_norm, w_in_b, kv_norm, w_kv, mem_norm, w_mem_kv, w_out, w_ffn_gate_up, w_ffn_down, w_router, w_exp_gate_up, w_exp_down):
    bsz, t, d = x.shape
    n = bsz * t
    m = mem.shape[1]
    depth = attn_norm_pre.shape[0]
    n_a = w_in_a.shape[0]
    hgrn_w = hgrn_lower_bounds.shape[1]
    mem_w = w_mem_kv.shape[2] // 2
    sb_w = w_kv.shape[1] // 2
    lower_bounds = jnp.cumsum(jax.nn.softmax(hgrn_lower_bounds.astype(F32), axis=0), axis=0)

    h = x.reshape(n, d)
    mem2 = mem.reshape(bsz * m, d)
    sb_kv = None
    for layer in range(depth):
        (mkv,) = norm_matmul(mem2, mem_norm[layer], w_mem_kv[layer].astype(BF16),
                             [2 * mem_w], [BF16], 512, "mem_kv_proj")
        mkv = mkv.reshape(bsz, m, 2 * mem_w)
        if layer < n_a:
            q, f, i, og, mq = norm_matmul(
                h, attn_norm_pre[layer], w_in_a[layer].astype(BF16),
                [hgrn_w, hgrn_w, hgrn_w, hgrn_w, mem_w], [BF16, F32, BF16, BF16, BF16], 512, "in_proj_a")
            r3 = lambda a: a.reshape(bsz, t, a.shape[-1])
            o_main = hgrn_mix(r3(q), r3(f), r3(i), r3(og), lower_bounds[layer], hgrn_out_norm[layer], 512)
        else:
            sq, mq = norm_matmul(h, attn_norm_pre[layer], w_in_b[layer - n_a].astype(BF16),
                                 [sb_w, mem_w], [BF16, BF16], 512, "in_proj_b")
            o_main = sb_attention(sq.reshape(bsz, t, sb_w), sb_kv, 256)
        h = mix_out(o_main, mq.reshape(bsz, t, mem_w), mkv, w_out[layer].astype(BF16),
                    h.reshape(bsz, t, d), attn_norm_post[layer], 512).reshape(n, d)
        if layer % 2 == 0:
            h = dense_ffn(h, ffn_norm_pre[layer], w_ffn_gate_up[layer // 2].astype(BF16),
                          w_ffn_down[layer // 2].astype(BF16), ffn_norm_post[layer], 512)
        else:
            h = moe_ffn(h, ffn_norm_pre[layer], w_router[layer // 2], w_exp_gate_up[layer // 2].astype(BF16),
                        w_exp_down[layer // 2].astype(BF16), ffn_norm_post[layer], 512)
        if layer == n_a - 1:
            (sb_kv,) = norm_matmul(h, kv_norm, w_kv.astype(BF16), [2 * sb_w], [BF16], 512, "kv_proj")
            sb_kv = sb_kv.reshape(bsz, t, 2 * sb_w)
    return h.reshape(bsz, t, d)
```
